```python
import math
import jax, jax.numpy as jnp
from jax import lax
import numpy as np

D_MODEL = 2048
BATCH = 2
SEQ = 16384
DEPTH = 2

N_A_LAYERS = DEPTH // 2
N_B_LAYERS = DEPTH - N_A_LAYERS

SSM_EXPAND = 2
D_INNER = SSM_EXPAND * D_MODEL
SSM_HEADDIM = 64
N_SSM_HEADS = D_INNER // SSM_HEADDIM
N_SSM_GROUPS = 8
HEADS_PER_GROUP = N_SSM_HEADS // N_SSM_GROUPS
D_STATE = 128
CONV_W = 4
SSD_CHUNK = 128
GN = N_SSM_GROUPS * D_STATE
CONV_DIM = D_INNER + 2 * GN
D_IN_PROJ = D_INNER + CONV_DIM + N_SSM_HEADS

HEAD_DIM = 64
N_Q_HEADS = D_MODEL // HEAD_DIM
N_KV_HEADS = 8
Q_PER_KV = N_Q_HEADS // N_KV_HEADS
WINDOW = 128
KV_WIDTH = N_KV_HEADS * HEAD_DIM
ROPE_THETA = 10000.0
ATTN_SCALE = 1.0 / math.sqrt(HEAD_DIM)

D_FF = 4 * D_MODEL

DEEPNORM_ALPHA = (2.0 * DEPTH) ** 0.25
DEEPNORM_BETA = (8.0 * DEPTH) ** -0.25

LN_EPS = 1e-5
RMS_EPS = 1e-5

kernel_name = 'yoco_ssd_swa_sink_hybrid'


def _layernorm(x, g, b):
    xf = x.astype(jnp.float32)
    mu = jnp.mean(xf, axis=-1, keepdims=True)
    var = jnp.mean(jnp.square(xf - mu), axis=-1, keepdims=True)
    y = (xf - mu) * lax.rsqrt(var + LN_EPS) * g.astype(jnp.float32) + b.astype(jnp.float32)
    return y.astype(x.dtype)


def _ssd_chunked_scan(xs, dt, A, Bm, Cm):
    bsz, length = xs.shape[0], xs.shape[1]
    n_chunks = length // SSD_CHUNK

    def to_chunks(t):
        return jnp.moveaxis(t.reshape(bsz, n_chunks, SSD_CHUNK, *t.shape[2:]), 1, 0)

    xdt = xs.astype(jnp.float32) * dt[..., None]
    dA = dt * A
    causal = jnp.tril(jnp.ones((SSD_CHUNK, SSD_CHUNK), dtype=bool))

    def step(state, inp):
        xdt_c, dA_c, b_c, c_c = inp
        cs = jnp.cumsum(dA_c, axis=1)
        seg = cs[:, :, None] - cs[:, None, :]
        decay = jnp.exp(jnp.where(causal[None, :, :, None, None], seg, -jnp.inf))
        cb = jnp.einsum('bign,bjgn->bgij', c_c, b_c)
        y_intra = jnp.einsum('bgij,bijgh,bjghp->bighp', cb, decay, xdt_c)
        y_inter = jnp.einsum('bign,bghpn->bighp', c_c, state) * jnp.exp(cs)[..., None]
        to_end = jnp.exp(cs[:, -1:] - cs)
        new_state = (state * jnp.exp(cs[:, -1])[..., None, None]
                     + jnp.einsum('bjgn,bjgh,bjghp->bghpn', b_c, to_end, xdt_c))
        return new_state, y_intra + y_inter

    state0 = jnp.zeros((bsz, N_SSM_GROUPS, HEADS_PER_GROUP, SSM_HEADDIM, D_STATE), jnp.float32)
    _, y = lax.scan(step, state0, (to_chunks(xdt), to_chunks(dA),
                                   to_chunks(Bm.astype(jnp.float32)),
                                   to_chunks(Cm.astype(jnp.float32))))
    return jnp.moveaxis(y, 0, 1).reshape(bsz, length, N_SSM_GROUPS, HEADS_PER_GROUP, SSM_HEADDIM)


def _mamba2_mixer(h, w_in, conv_w, conv_b, dt_bias, A_log, D_skip, norm_w, w_out):
    bsz, length, _ = h.shape
    zxbcdt = h @ w_in
    z = zxbcdt[..., :D_INNER]
    xbc = zxbcdt[..., D_INNER:D_INNER + CONV_DIM]
    dt_raw = zxbcdt[..., D_INNER + CONV_DIM:]
    xbc = lax.conv_general_dilated(
        xbc, conv_w[:, None, :].astype(xbc.dtype), window_strides=(1,),
        padding=[(CONV_W - 1, 0)], dimension_numbers=('NWC', 'WIO', 'NWC'),
        feature_group_count=CONV_DIM)
    xbc = jax.nn.silu(xbc + conv_b)
    xs = xbc[..., :D_INNER].reshape(bsz, length, N_SSM_GROUPS, HEADS_PER_GROUP, SSM_HEADDIM)
    Bm = xbc[..., D_INNER:D_INNER + GN].reshape(bsz, length, N_SSM_GROUPS, D_STATE)
    Cm = xbc[..., D_INNER + GN:].reshape(bsz, length, N_SSM_GROUPS, D_STATE)
    dt = jax.nn.softplus(dt_raw.astype(jnp.float32) + dt_bias.astype(jnp.float32))
    dt = dt.reshape(bsz, length, N_SSM_GROUPS, HEADS_PER_GROUP)
    A = -jnp.exp(A_log.astype(jnp.float32)).reshape(N_SSM_GROUPS, HEADS_PER_GROUP)
    y = _ssd_chunked_scan(xs, dt, A, Bm, Cm)
    y = y + D_skip.astype(jnp.float32).reshape(N_SSM_GROUPS, HEADS_PER_GROUP)[..., None] * xs.astype(jnp.float32)
    y = y.reshape(bsz, length, D_INNER) * jax.nn.silu(z.astype(jnp.float32))
    yg = y.reshape(bsz, length, N_SSM_GROUPS, D_INNER // N_SSM_GROUPS)
    yg = yg * lax.rsqrt(jnp.mean(jnp.square(yg), axis=-1, keepdims=True) + RMS_EPS)
    y = yg.reshape(bsz, length, D_INNER) * norm_w.astype(jnp.float32)
    return y.astype(h.dtype) @ w_out


def _rope_tables(length):
    inv_freq = ROPE_THETA ** (-jnp.arange(0, HEAD_DIM, 2, dtype=jnp.float32) / HEAD_DIM)
    ang = jnp.arange(length, dtype=jnp.float32)[:, None] * inv_freq[None, :]
    return jnp.cos(ang), jnp.sin(ang)


def _rope(t, cos, sin):
    tf = t.astype(jnp.float32)
    t1, t2 = tf[..., :HEAD_DIM // 2], tf[..., HEAD_DIM // 2:]
    c, s = cos[None, :, None, :], sin[None, :, None, :]
    return jnp.concatenate([t1 * c - t2 * s, t2 * c + t1 * s], axis=-1).astype(t.dtype)


def _shared_kv(h, kv_w, cos, sin):
    bsz, length, _ = h.shape
    kv = h @ kv_w
    k = kv[..., :KV_WIDTH].reshape(bsz, length, N_KV_HEADS, HEAD_DIM)
    v = kv[..., KV_WIDTH:].reshape(bsz, length, N_KV_HEADS, HEAD_DIM)
    return _rope(k, cos, sin), v


def _swa_sink_attention(h, w_q, sinks, w_o, k, v, cos, sin):
    bsz, length, _ = h.shape
    n_blocks = length // WINDOW
    q = _rope((h @ w_q).reshape(bsz, length, N_Q_HEADS, HEAD_DIM), cos, sin)
    qb = jnp.moveaxis(q.reshape(bsz, n_blocks, WINDOW, N_KV_HEADS, Q_PER_KV, HEAD_DIM), 1, 0)

    def band(t):
        tp = jnp.pad(t, ((0, 0), (WINDOW, 0), (0, 0), (0, 0)))
        prev = tp[:, :length].reshape(bsz, n_blocks, WINDOW, N_KV_HEADS, HEAD_DIM)
        cur = t.reshape(bsz, n_blocks, WINDOW, N_KV_HEADS, HEAD_DIM)
        return jnp.moveaxis(jnp.concatenate([prev, cur], axis=2), 1, 0)

    kb, vb = band(k), band(v)
    qi = jnp.arange(WINDOW)[:, None]
    kc = jnp.arange(2 * WINDOW)[None, :]
    dist = qi + WINDOW - kc
    in_band = (dist >= 0) & (dist < WINDOW)
    sink = sinks.astype(jnp.float32).reshape(N_KV_HEADS, Q_PER_KV)[None, :, :, None, None]

    def one_block(args):
        q_blk, k_blk, v_blk, blk_idx = args
        s = jnp.einsum('bqkgd,bckd->bkgqc', q_blk, k_blk).astype(jnp.float32) * ATTN_SCALE
        valid = in_band & ((blk_idx > 0) | (kc >= WINDOW))
        s = jnp.where(valid, s, -jnp.inf)
        m = jnp.maximum(jnp.max(s, axis=-1, keepdims=True), sink)
        p = jnp.exp(s - m)
        w = p / (jnp.sum(p, axis=-1, keepdims=True) + jnp.exp(sink - m))
        return jnp.einsum('bkgqc,bckd->bqkgd', w.astype(v_blk.dtype), v_blk)

    o = lax.map(one_block, (qb, kb, vb, jnp.arange(n_blocks)))
    o = jnp.moveaxis(o, 0, 1).reshape(bsz, length, N_Q_HEADS * HEAD_DIM)
    return o @ w_o


def _sqrelu_mlp(h, w1, w2):
    return jnp.square(jax.nn.relu(h @ w1)) @ w2


def setup_inputs(seed: int = 0) -> dict:
    key = jax.random.key(seed)
    ks = jax.random.split(key, 20)
    f32 = jnp.float32
    nrm = lambda k, shape, scale: jax.random.normal(k, shape, f32) * scale
    x = jax.random.normal(ks[0], (BATCH, SEQ, D_MODEL), f32)
    a_w_in = nrm(ks[1], (N_A_LAYERS, D_MODEL, D_IN_PROJ), D_MODEL ** -0.5)
    a_conv_w = nrm(ks[2], (N_A_LAYERS, CONV_W, CONV_DIM), CONV_W ** -0.5)
    a_conv_b = nrm(ks[3], (N_A_LAYERS, CONV_DIM), 0.02)
    dt0 = jnp.exp(jax.random.uniform(ks[4], (N_A_LAYERS, N_SSM_HEADS), f32,
                                     math.log(1e-3), math.log(1e-1)))
    a_dt_bias = dt0 + jnp.log(-jnp.expm1(-dt0))
    a_A_log = jnp.log(jax.random.uniform(ks[5], (N_A_LAYERS, N_SSM_HEADS), f32, 1.0, 16.0))
    a_D = 1.0 + nrm(ks[6], (N_A_LAYERS, N_SSM_HEADS), 0.1)
    a_norm_w = 1.0 + nrm(ks[7], (N_A_LAYERS, D_INNER), 0.02)
    a_w_out = nrm(ks[8], (N_A_LAYERS, D_INNER, D_MODEL), DEEPNORM_BETA * D_INNER ** -0.5)
    kv_w = jnp.concatenate([
        nrm(ks[9], (D_MODEL, KV_WIDTH), D_MODEL ** -0.5),
        nrm(ks[10], (D_MODEL, KV_WIDTH), DEEPNORM_BETA * D_MODEL ** -0.5)], axis=1)
    b_w_q = nrm(ks[11], (N_B_LAYERS, D_MODEL, N_Q_HEADS * HEAD_DIM), D_MODEL ** -0.5)
    b_sinks = nrm(ks[12], (N_B_LAYERS, N_Q_HEADS), 0.5)
    b_w_o = nrm(ks[13], (N_B_LAYERS, N_Q_HEADS * HEAD_DIM, D_MODEL),
                DEEPNORM_BETA * (N_Q_HEADS * HEAD_DIM) ** -0.5)
    mlp_w1 = nrm(ks[14], (DEPTH, D_MODEL, D_FF), DEEPNORM_BETA * D_MODEL ** -0.5)
    mlp_w2 = nrm(ks[15], (DEPTH, D_FF, D_MODEL), DEEPNORM_BETA * D_FF ** -0.5)
    ln_g = 1.0 + nrm(ks[16], (DEPTH, 2, D_MODEL), 0.02)
    ln_b = nrm(ks[17], (DEPTH, 2, D_MODEL), 0.02)
    return {'x': x, 'a_w_in': a_w_in, 'a_conv_w': a_conv_w, 'a_conv_b': a_conv_b,
            'a_dt_bias': a_dt_bias, 'a_A_log': a_A_log, 'a_D': a_D, 'a_norm_w': a_norm_w,
            'a_w_out': a_w_out, 'kv_w': kv_w, 'b_w_q': b_w_q, 'b_sinks': b_sinks,
            'b_w_o': b_w_o, 'mlp_w1': mlp_w1, 'mlp_w2': mlp_w2, 'ln_g': ln_g, 'ln_b': ln_b}


def reference(x, a_w_in, a_conv_w, a_conv_b, a_dt_bias, a_A_log, a_D, a_norm_w, a_w_out,
              kv_w, b_w_q, b_sinks, b_w_o, mlp_w1, mlp_w2, ln_g, ln_b):
    length = x.shape[1]
    cos, sin = _rope_tables(length)
    h = x
    k_shared, v_shared = None, None
    for layer in range(DEPTH):
        if layer < N_A_LAYERS:
            mix = _mamba2_mixer(h, a_w_in[layer], a_conv_w[layer], a_conv_b[layer],
                                a_dt_bias[layer], a_A_log[layer], a_D[layer],
                                a_norm_w[layer], a_w_out[layer])
        else:
            if layer == N_A_LAYERS:
                k_shared, v_shared = _shared_kv(h, kv_w, cos, sin)
            j = layer - N_A_LAYERS
            mix = _swa_sink_attention(h, b_w_q[j], b_sinks[j], b_w_o[j],
                                      k_shared, v_shared, cos, sin)
        h = _layernorm(DEEPNORM_ALPHA * h + mix, ln_g[layer, 0], ln_b[layer, 0])
        h = _layernorm(DEEPNORM_ALPHA * h + _sqrelu_mlp(h, mlp_w1[layer], mlp_w2[layer]),
                       ln_g[layer, 1], ln_b[layer, 1])
    return h
```

```python
import functools
import math

import jax
import jax.numpy as jnp
from jax import lax
from jax.experimental import pallas as pl
from jax.experimental.pallas import tpu as pltpu

F32 = jnp.float32
BF16 = jnp.bfloat16

D_MODEL = 2048
DEPTH = 2
N_A_LAYERS = DEPTH // 2

D_INNER = 2 * D_MODEL
SSM_HEADDIM = 64
N_SSM_HEADS = D_INNER // SSM_HEADDIM
N_SSM_GROUPS = 8
HEADS_PER_GROUP = N_SSM_HEADS // N_SSM_GROUPS
D_STATE = 128
CONV_W = 4
CHUNK = 128
GN = N_SSM_GROUPS * D_STATE
GROUP_W = D_INNER // N_SSM_GROUPS
ZX_W = 2 * D_INNER + 2 * GN
DT_PAD = 128
IN_PROJ_W = ZX_W + DT_PAD

HEAD_DIM = 64
N_Q_HEADS = D_MODEL // HEAD_DIM
N_KV_HEADS = 8
Q_PER_KV = N_Q_HEADS // N_KV_HEADS
WINDOW = 128
KV_WIDTH = N_KV_HEADS * HEAD_DIM
QKV_W = D_MODEL + 2 * KV_WIDTH
ROPE_THETA = 10000.0
ATTN_SCALE = 1.0 / math.sqrt(HEAD_DIM)

D_FF = 4 * D_MODEL
DEEPNORM_ALPHA = (2.0 * DEPTH) ** 0.25
LN_EPS = 1e-5
RMS_EPS = 1e-5

LANES = 128
VMEM_LIMIT = 56 * 1024 * 1024


def _params(sem):
    return pltpu.CompilerParams(dimension_semantics=sem, vmem_limit_bytes=VMEM_LIMIT)


def _layernorm(v, g, b):
    mu = jnp.mean(v, axis=-1, keepdims=True)
    d = v - mu
    var = jnp.mean(d * d, axis=-1, keepdims=True)
    return d * lax.rsqrt(var + LN_EPS) * g + b


def _split3(x):
    hi = x.astype(BF16)
    r = x - hi.astype(F32)
    mid = r.astype(BF16)
    lo = (r - mid.astype(F32)).astype(BF16)
    return hi, mid, lo


def _proj_kernel(x_ref, w_ref, o_ref, xb_ref):
    @pl.when(pl.program_id(1) == 0)
    def _():
        xb_ref[...] = x_ref[...].astype(BF16)

    o_ref[...] = jnp.dot(xb_ref[...], w_ref[...], preferred_element_type=F32).astype(o_ref.dtype)


def _proj(x, w, *, tm, tn, out_dtype):
    m, k = x.shape
    n = w.shape[1]
    return pl.pallas_call(
        _proj_kernel,
        grid=(m // tm, n // tn),
        in_specs=[pl.BlockSpec((tm, k), lambda i, j: (i, 0)),
                  pl.BlockSpec((k, tn), lambda i, j: (0, j))],
        out_specs=pl.BlockSpec((tm, tn), lambda i, j: (i, j)),
        out_shape=jax.ShapeDtypeStruct((m, n), out_dtype),
        scratch_shapes=[pltpu.VMEM((tm, k), BF16)],
        compiler_params=_params(("parallel", "arbitrary")),
        name="proj",
    )(x, w)


def _qkv_kernel(n_rope_tiles, x_ref, w_ref, cos_ref, sa_ref, sb_ref, o_ref, xb_ref):
    j = pl.program_id(1)

    @pl.when(j == 0)
    def _():
        xb_ref[...] = x_ref[...].astype(BF16)

    acc = jnp.dot(xb_ref[...], w_ref[...], preferred_element_type=F32)
    tn = acc.shape[1]

    @pl.when(j < n_rope_tiles)
    def _():
        cos = cos_ref[...]
        sa = sa_ref[...]
        sb = sb_ref[...]
        for c in range(tn // LANES):
            blk = acc[:, c * LANES:(c + 1) * LANES]
            up = pltpu.roll(blk, LANES - HEAD_DIM // 2, axis=1)
            dn = pltpu.roll(blk, HEAD_DIM // 2, axis=1)
            o_ref[:, c * LANES:(c + 1) * LANES] = (blk * cos + up * sa + dn * sb).astype(o_ref.dtype)

    @pl.when(j >= n_rope_tiles)
    def _():
        o_ref[...] = acc.astype(o_ref.dtype)


def _qkv_proj(x, w, cos, sa, sb, *, seq, tm, tn):
    m, k = x.shape
    n = w.shape[1]
    n_rope_tiles = (D_MODEL + KV_WIDTH) // tn
    tab = pl.BlockSpec((tm, LANES), lambda i, j: (i % (seq // tm), 0))
    return pl.pallas_call(
        functools.partial(_qkv_kernel, n_rope_tiles),
        grid=(m // tm, n // tn),
        in_specs=[pl.BlockSpec((tm, k), lambda i, j: (i, 0)),
                  pl.BlockSpec((k, tn), lambda i, j: (0, j)),
                  tab, tab, tab],
        out_specs=pl.BlockSpec((tm, tn), lambda i, j: (i, j)),
        out_shape=jax.ShapeDtypeStruct((m, n), BF16),
        scratch_shapes=[pltpu.VMEM((tm, k), BF16)],
        compiler_params=_params(("parallel", "arbitrary")),
        name="qkv_rope",
    )(x, w, cos, sa, sb)


def _mm_ln_kernel(x_ref, w_ref, res_ref, g_ref, b_ref, o_ref, acc_ref):
    kk = pl.program_id(1)

    @pl.when(kk == 0)
    def _():
        acc_ref[...] = jnp.zeros_like(acc_ref)

    acc_ref[...] += jnp.dot(x_ref[...], w_ref[...], preferred_element_type=F32)

    @pl.when(kk == pl.num_programs(1) - 1)
    def _():
        v = DEEPNORM_ALPHA * res_ref[...] + acc_ref[...]
        o_ref[...] = _layernorm(v, g_ref[...], b_ref[...])


def _mm_res_ln(x, w, res, g, b, *, tm, tk):
    m, k = x.shape
    n = w.shape[1]
    return pl.pallas_call(
        _mm_ln_kernel,
        grid=(m // tm, k // tk),
        in_specs=[pl.BlockSpec((tm, tk), lambda i, kk: (i, kk)),
                  pl.BlockSpec((tk, n), lambda i, kk: (kk, 0)),
                  pl.BlockSpec((tm, n), lambda i, kk: (i, 0)),
                  pl.BlockSpec((1, n), lambda i, kk: (0, 0)),
                  pl.BlockSpec((1, n), lambda i, kk: (0, 0))],
        out_specs=pl.BlockSpec((tm, n), lambda i, kk: (i, 0)),
        out_shape=jax.ShapeDtypeStruct((m, n), F32),
        scratch_shapes=[pltpu.VMEM((tm, n), F32)],
        compiler_params=_params(("parallel", "arbitrary")),
        name="mm_res_ln",
    )(x, w, res, g, b)


def _mlp_kernel(x_ref, w1_ref, w2_ref, g_ref, b_ref, o_ref, xb_ref, acc_ref):
    f = pl.program_id(1)

    @pl.when(f == 0)
    def _():
        xb_ref[...] = x_ref[...].astype(BF16)
        acc_ref[...] = jnp.zeros_like(acc_ref)

    hid = jnp.dot(xb_ref[...], w1_ref[...], preferred_element_type=F32)
    hid = jnp.square(jnp.maximum(hid, 0.0)).astype(BF16)
    acc_ref[...] += jnp.dot(hid, w2_ref[...], preferred_element_type=F32)

    @pl.when(f == pl.num_programs(1) - 1)
    def _():
        v = DEEPNORM_ALPHA * x_ref[...] + acc_ref[...]
        o_ref[...] = _layernorm(v, g_ref[...], b_ref[...])


def _mlp_ln(x, w1, w2, g, b, *, tm, tf):
    m, d = x.shape
    dff = w1.shape[1]
    return pl.pallas_call(
        _mlp_kernel,
        grid=(m // tm, dff // tf),
        in_specs=[pl.BlockSpec((tm, d), lambda i, f: (i, 0)),
                  pl.BlockSpec((d, tf), lambda i, f: (0, f)),
                  pl.BlockSpec((tf, d), lambda i, f: (f, 0)),
                  pl.BlockSpec((1, d), lambda i, f: (0, 0)),
                  pl.BlockSpec((1, d), lambda i, f: (0, 0))],
        out_specs=pl.BlockSpec((tm, d), lambda i, f: (i, 0)),
        out_shape=jax.ShapeDtypeStruct((m, d), F32),
        scratch_shapes=[pltpu.VMEM((tm, d), BF16), pltpu.VMEM((tm, d), F32)],
        compiler_params=_params(("parallel", "arbitrary")),
        name="mlp_ln",
    )(x, w1, w2, g, b)


def _silu(v):
    return v * (1.0 / (1.0 + jnp.exp(-v)))


def _ssd_kernel(z_ref, xs_ref, bm_ref, cm_ref, dt_ref,
                cwx_ref, cwb_ref, cwc_ref, cbx_ref, cbb_ref, cbc_ref,
                dtb_ref, alog_ref, dexp_ref, nw_ref, e3_ref, r0_ref,
                o_ref,
                xp_x, xp_b, xp_c, xc_x, xc_b, xc_c, state_ref, dtx_ref, dte_ref, ecs_ref):
    c = pl.program_id(1)

    @pl.when(c == 0)
    def _():
        xp_x[0:8, :] = jnp.zeros((8, D_INNER), F32)
        xp_b[0:8, :] = jnp.zeros((8, GN), F32)
        xp_c[0:8, :] = jnp.zeros((8, GN), F32)
        state_ref[...] = jnp.zeros_like(state_ref)

    def conv(src_ref, xp_ref, cw_ref, cb_ref, dst_ref):
        xp_ref[8:8 + CHUNK, :] = src_ref[...]
        acc = cb_ref[...] + cw_ref[CONV_W - 1:CONV_W, :] * xp_ref[8:8 + CHUNK, :]
        for w in range(CONV_W - 1):
            off = 8 - (CONV_W - 1) + w
            acc = acc + cw_ref[w:w + 1, :] * xp_ref[off:off + CHUNK, :]
        dst_ref[...] = _silu(acc)
        xp_ref[0:8, :] = xp_ref[CHUNK:CHUNK + 8, :]

    conv(xs_ref, xp_x, cwx_ref, cbx_ref, xc_x)
    conv(bm_ref, xp_b, cwb_ref, cbb_ref, xc_b)
    conv(cm_ref, xp_c, cwc_ref, cbc_ref, xc_c)

    pre = dt_ref[...] + dtb_ref[...]
    dt = jnp.maximum(pre, 0.0) + jnp.log1p(jnp.exp(-jnp.abs(pre)))
    a_neg = -jnp.exp(alog_ref[...])
    da = dt * a_neg
    row = lax.broadcasted_iota(jnp.int32, (CHUNK, CHUNK), 0)
    col = lax.broadcasted_iota(jnp.int32, (CHUNK, CHUNK), 1)
    causal = row >= col
    tril = jnp.where(causal, 1.0, 0.0).astype(BF16)
    d_hi, d_mid, d_lo = _split3(da)
    cs = (jnp.dot(tril, d_hi, preferred_element_type=F32)
          + jnp.dot(tril, d_mid, preferred_element_type=F32)
          + jnp.dot(tril, d_lo, preferred_element_type=F32))
    cs_t = cs.T
    cs_last = cs[CHUNK - 1:CHUNK, :]
    to_end = jnp.exp(cs_last - cs)
    ecs = jnp.exp(cs)

    def expand(v):
        parts = jnp.concatenate(_split3(v), axis=1)
        return jnp.dot(parts, e3_ref[...], preferred_element_type=F32)

    dtx_ref[...] = expand(dt)
    dte_ref[...] = expand(dt * to_end)
    ecs_ref[...] = expand(ecs)
    cs3 = _split3(cs)

    for g in range(N_SSM_GROUPS):
        lo, hi = g * GROUP_W, (g + 1) * GROUP_W
        xs_g = xc_x[:, lo:hi]
        b_g = xc_b[:, g * D_STATE:(g + 1) * D_STATE]
        c_g = xc_c[:, g * D_STATE:(g + 1) * D_STATE].astype(BF16)
        xdt_g = (xs_g * dtx_ref[:, lo:hi]).astype(BF16)
        xdte_g = (xs_g * dte_ref[:, lo:hi]).astype(BF16)
        cb = lax.dot_general(c_g, b_g.astype(BF16), (((1,), (1,)), ((), ())),
                             preferred_element_type=F32)
        shift = (LANES - g * HEADS_PER_GROUP) % LANES
        parts = jnp.concatenate(
            [p if shift == 0 else pltpu.roll(p.astype(F32), shift, axis=1).astype(BF16) for p in cs3],
            axis=1)
        rep = jnp.dot(parts, r0_ref[...], preferred_element_type=F32)
        st = state_ref[g]
        y_g = jnp.dot(c_g, st.astype(BF16), preferred_element_type=F32) * ecs_ref[:, lo:hi]
        pieces = []
        for hh in range(HEADS_PER_GROUP):
            h = g * HEADS_PER_GROUP + hh
            seg = rep[:, hh * LANES:(hh + 1) * LANES] - cs_t[h:h + 1, :]
            decay = jnp.exp(jnp.where(causal, seg, -jnp.inf))
            mm = (cb * decay).astype(BF16)
            pieces.append(jnp.dot(mm, xdt_g[:, hh * SSM_HEADDIM:(hh + 1) * SSM_HEADDIM],
                                  preferred_element_type=F32))
        y_g = y_g + jnp.concatenate(pieces, axis=1)
        state_ref[g] = (st * ecs_ref[CHUNK - 1:CHUNK, lo:hi]
                        + jnp.dot(b_g.T.astype(BF16), xdte_g, preferred_element_type=F32))
        y_g = y_g + dexp_ref[:, lo:hi] * xs_g
        y_g = y_g * _silu(z_ref[:, lo:hi])
        ms = jnp.mean(y_g * y_g, axis=-1, keepdims=True)
        o_ref[:, lo:hi] = (y_g * lax.rsqrt(ms + RMS_EPS) * nw_ref[:, lo:hi]).astype(o_ref.dtype)


def _ssd(zx, cwx, cwb, cwc, cbx, cbb, cbc, dtb, alog, dexp, nw, e3, r0, *, bsz, seq):
    nc = seq // CHUNK
    rows = lambda b, c: b * nc + c

    def col(width, idx):
        return pl.BlockSpec((CHUNK, width), lambda b, c: (rows(b, c), idx))

    def full(a):
        return pl.BlockSpec(a.shape, lambda b, c: (0,) * a.ndim)

    consts = (cwx, cwb, cwc, cbx, cbb, cbc, dtb, alog, dexp, nw, e3, r0)
    return pl.pallas_call(
        _ssd_kernel,
        grid=(bsz, nc),
        in_specs=[col(D_INNER, 0), col(D_INNER, 1), col(GN, 2 * D_INNER // GN),
                  col(GN, 2 * D_INNER // GN + 1), col(DT_PAD, ZX_W // DT_PAD)]
                 + [full(a) for a in consts],
        out_specs=pl.BlockSpec((CHUNK, D_INNER), lambda b, c: (rows(b, c), 0)),
        out_shape=jax.ShapeDtypeStruct((bsz * seq, D_INNER), BF16),
        scratch_shapes=[pltpu.VMEM((CHUNK + 8, D_INNER), F32),
                        pltpu.VMEM((CHUNK + 8, GN), F32),
                        pltpu.VMEM((CHUNK + 8, GN), F32),
                        pltpu.VMEM((CHUNK, D_INNER), F32),
                        pltpu.VMEM((CHUNK, GN), F32),
                        pltpu.VMEM((CHUNK, GN), F32),
                        pltpu.VMEM((N_SSM_GROUPS, D_STATE, GROUP_W), F32),
                        pltpu.VMEM((CHUNK, D_INNER), F32),
                        pltpu.VMEM((CHUNK, D_INNER), F32),
                        pltpu.VMEM((CHUNK, D_INNER), F32)],
        compiler_params=_params(("parallel", "arbitrary")),
        name="ssd",
    )(zx, zx, zx, zx, zx, *consts)


def _attn_kernel(sink_ref, q_ref, kp_ref, kc_ref, vp_ref, vc_ref, o_ref):
    blk = pl.program_id(1)
    qi = lax.broadcasted_iota(jnp.int32, (WINDOW, 2 * WINDOW), 0)
    kc = lax.broadcasted_iota(jnp.int32, (WINDOW, 2 * WINDOW), 1)
    dist = qi + WINDOW - kc
    valid = (dist >= 0) & (dist < WINDOW) & ((blk > 0) | (kc >= WINDOW))
    for g in range(N_KV_HEADS):
        lo, hi = g * HEAD_DIM, (g + 1) * HEAD_DIM
        kb = jnp.concatenate([kp_ref[:, lo:hi], kc_ref[:, lo:hi]], axis=0)
        vb = jnp.concatenate([vp_ref[:, lo:hi], vc_ref[:, lo:hi]], axis=0)
        for i in range(Q_PER_KV):
            h = g * Q_PER_KV + i
            q_h = q_ref[:, h * HEAD_DIM:(h + 1) * HEAD_DIM]
            s = lax.dot_general(q_h, kb, (((1,), (1,)), ((), ())),
                                preferred_element_type=F32) * ATTN_SCALE
            s = jnp.where(valid, s, -jnp.inf)
            sink = sink_ref[h]
            m = jnp.maximum(jnp.max(s, axis=-1, keepdims=True), sink)
            p = jnp.exp(s - m)
            denom = jnp.sum(p, axis=-1, keepdims=True) + jnp.exp(sink - m)
            o_h = jnp.dot(p.astype(BF16), vb, preferred_element_type=F32) / denom
            o_ref[:, h * HEAD_DIM:(h + 1) * HEAD_DIM] = o_h.astype(o_ref.dtype)


def _attention(qkv, sinks, *, bsz, seq):
    nb = seq // WINDOW
    rows = lambda b, i: b * nb + i
    prev = lambda b, i: b * nb + jnp.maximum(i - 1, 0)
    k_idx = D_MODEL // KV_WIDTH
    return pl.pallas_call(
        _attn_kernel,
        grid=(bsz, nb),
        in_specs=[pl.BlockSpec(memory_space=pltpu.SMEM),
                  pl.BlockSpec((WINDOW, D_MODEL), lambda b, i: (rows(b, i), 0)),
                  pl.BlockSpec((WINDOW, KV_WIDTH), lambda b, i: (prev(b, i), k_idx)),
                  pl.BlockSpec((WINDOW, KV_WIDTH), lambda b, i: (rows(b, i), k_idx)),
                  pl.BlockSpec((WINDOW, KV_WIDTH), lambda b, i: (prev(b, i), k_idx + 1)),
                  pl.BlockSpec((WINDOW, KV_WIDTH), lambda b, i: (rows(b, i), k_idx + 1))],
        out_specs=pl.BlockSpec((WINDOW, D_MODEL), lambda b, i: (rows(b, i), 0)),
        out_shape=jax.ShapeDtypeStruct((bsz * seq, D_MODEL), BF16),
        compiler_params=_params(("parallel", "arbitrary")),
        name="swa_attn",
    )(sinks, qkv, qkv, qkv, qkv, qkv)


def _rope_tables(seq):
    inv_freq = ROPE_THETA ** (-jnp.arange(0, HEAD_DIM, 2, dtype=F32) / HEAD_DIM)
    ang = jnp.arange(seq, dtype=F32)[:, None] * inv_freq[None, :]
    cos, sin = jnp.cos(ang), jnp.sin(ang)
    zero = jnp.zeros_like(sin)
    cos_t = jnp.tile(cos, (1, LANES // (HEAD_DIM // 2)))
    sa = jnp.tile(jnp.concatenate([-sin, zero], axis=1), (1, LANES // HEAD_DIM))
    sb = jnp.tile(jnp.concatenate([zero, sin], axis=1), (1, LANES // HEAD_DIM))
    return cos_t, sa, sb


def _expand_tables():
    head = jnp.arange(LANES)
    chan_head = jnp.arange(D_INNER) // SSM_HEADDIM
    e = (head[:, None] == chan_head[None, :]).astype(BF16)
    e3 = jnp.tile(e, (3, 1))
    rep_head = jnp.arange(HEADS_PER_GROUP * LANES) // LANES
    r = (head[:, None] == rep_head[None, :]).astype(BF16)
    r0 = jnp.tile(r, (3, 1))
    return e3, r0


def _pad_lanes(v, fill):
    return jnp.concatenate([v, jnp.full((LANES - v.shape[0],), fill, F32)])[None, :]


def kernel(x, a_w_in, a_conv_w, a_conv_b, a_dt_bias, a_A_log, a_D, a_norm_w, a_w_out, kv_w, b_w_q, b_sinks, b_w_o, mlp_w1, mlp_w2, ln_g, ln_b):
    bsz, seq, d = x.shape
    t = bsz * seq
    h = x.reshape(t, d)
    cos_t, sa, sb = _rope_tables(seq)
    e3, r0 = _expand_tables()
    qkv_shared = None
    for layer in range(DEPTH):
        g0, b0 = ln_g[layer, 0][None, :], ln_b[layer, 0][None, :]
        g1, b1 = ln_g[layer, 1][None, :], ln_b[layer, 1][None, :]
        if layer < N_A_LAYERS:
            w_in = a_w_in[layer]
            w_in_p = jnp.concatenate(
                [w_in.astype(BF16), jnp.zeros((d, DT_PAD - N_SSM_HEADS), BF16)], axis=1)
            zx = _proj(h, w_in_p, tm=512, tn=1152, out_dtype=F32)
            cw, cb = a_conv_w[layer], a_conv_b[layer][None, :]
            y = _ssd(zx,
                     cw[:, :D_INNER], cw[:, D_INNER:D_INNER + GN], cw[:, D_INNER + GN:],
                     cb[:, :D_INNER], cb[:, D_INNER:D_INNER + GN], cb[:, D_INNER + GN:],
                     _pad_lanes(a_dt_bias[layer], 0.0), _pad_lanes(a_A_log[layer], 0.0),
                     jnp.repeat(a_D[layer], SSM_HEADDIM)[None, :], a_norm_w[layer][None, :],
                     e3, r0, bsz=bsz, seq=seq)
            h = _mm_res_ln(y, a_w_out[layer].astype(BF16), h, g0, b0, tm=512, tk=512)
        else:
            j = layer - N_A_LAYERS
            if layer == N_A_LAYERS:
                kv_b = kv_w.astype(BF16)
            w_qkv = jnp.concatenate([b_w_q[j].astype(BF16), kv_b], axis=1)
            qkv = _qkv_proj(h, w_qkv, cos_t, sa, sb, seq=seq, tm=512, tn=512)
            if layer == N_A_LAYERS:
                qkv_shared = qkv
            if qkv is not qkv_shared:
                qkv = jnp.concatenate([qkv[:, :D_MODEL], qkv_shared[:, D_MODEL:]], axis=1)
            o = _attention(qkv, b_sinks[j], bsz=bsz, seq=seq)
            h = _mm_res_ln(o, b_w_o[j].astype(BF16), h, g0, b0, tm=512, tk=512)
        h = _mlp_ln(h, mlp_w1[layer].astype(BF16), mlp_w2[layer].astype(BF16), g1, b1,
                    tm=512, tf=512)
    return h.reshape(bsz, seq, d)
```

```python
import functools
import math

import numpy as np
import jax
import jax.numpy as jnp
from jax import lax
from jax.experimental import pallas as pl
from jax.experimental.pallas import tpu as pltpu

F32 = jnp.float32
BF16 = jnp.bfloat16

D_MODEL = 2048
DEPTH = 2
N_A_LAYERS = DEPTH // 2

D_INNER = 2 * D_MODEL
SSM_HEADDIM = 64
N_SSM_HEADS = D_INNER // SSM_HEADDIM
N_SSM_GROUPS = 8
HEADS_PER_GROUP = N_SSM_HEADS // N_SSM_GROUPS
D_STATE = 128
CONV_W = 4
CHUNK = 128
GN = N_SSM_GROUPS * D_STATE
GROUP_W = D_INNER // N_SSM_GROUPS
CONV_DIM = D_INNER + 2 * GN
ZX_W = D_INNER + CONV_DIM

HEAD_DIM = 64
HALF = HEAD_DIM // 2
N_Q_HEADS = D_MODEL // HEAD_DIM
N_KV_HEADS = 8
Q_PER_KV = N_Q_HEADS // N_KV_HEADS
WINDOW = 128
KV_WIDTH = N_KV_HEADS * HEAD_DIM
ROPE_THETA = 10000.0
ATTN_SCALE = 1.0 / math.sqrt(HEAD_DIM)

D_FF = 4 * D_MODEL
DEEPNORM_ALPHA = (2.0 * DEPTH) ** 0.25
LN_EPS = 1e-5
RMS_EPS = 1e-5

LANES = 128
SUBLANES = 8
MXU_N = 256
VMEM_LIMIT = 56 * 1024 * 1024

K_DUP_W = N_KV_HEADS * LANES
QKV_W = D_MODEL + K_DUP_W + KV_WIDTH
QKV_SUB = 512
ROPE_SUBTILES = (D_MODEL + K_DUP_W) // QKV_SUB


def _params(sem):
    return pltpu.CompilerParams(dimension_semantics=sem, vmem_limit_bytes=VMEM_LIMIT)


def _resident(shape):
    return pl.BlockSpec(shape, lambda *_: (0,) * len(shape), pipeline_mode=pl.Buffered(1))


def _layernorm(v, g, b):
    mu = jnp.mean(v, axis=-1, keepdims=True)
    d = v - mu
    var = jnp.mean(d * d, axis=-1, keepdims=True)
    return d * lax.rsqrt(var + LN_EPS) * g + b


def _silu(v):
    return v * (1.0 / (1.0 + jnp.exp(-v)))


def _split3(x):
    hi = x.astype(BF16).astype(F32)
    r = x - hi
    mid = r.astype(BF16).astype(F32)
    lo = (r - mid).astype(BF16).astype(F32)
    return hi, mid, lo


def _inproj_kernel(seq_tiles, n_gate_tiles, x_ref, w_ref, cw_ref, cb_ref, o_ref,
                   xb_ref, xp_ref, tail_ref):
    i = pl.program_id(0)
    j = pl.program_id(1)
    tm, tn = o_ref.shape

    @pl.when(j == 0)
    def _():
        xb_ref[...] = x_ref[...].astype(BF16)

    @pl.when(j < n_gate_tiles)
    def _():
        for n in range(tn // MXU_N):
            cols = slice(n * MXU_N, (n + 1) * MXU_N)
            acc = jnp.dot(xb_ref[...], w_ref[:, cols], preferred_element_type=F32)
            o_ref[:, cols] = _silu(acc)

    @pl.when(j >= n_gate_tiles)
    def _():
        jj = j - n_gate_tiles
        first = (i % seq_tiles) == 0
        for n in range(tn // MXU_N):
            cols = slice(n * MXU_N, (n + 1) * MXU_N)
            acc = jnp.dot(xb_ref[...], w_ref[:, cols], preferred_element_type=F32)
            xp_ref[0:SUBLANES, cols] = jnp.where(first, 0.0, tail_ref[jj, :, cols])
            xp_ref[SUBLANES:SUBLANES + tm, cols] = acc
            tail_ref[jj, :, cols] = acc[tm - SUBLANES:tm, :]
            out = cb_ref[:, cols] + cw_ref[CONV_W - 1:CONV_W, cols] * acc
            for w in range(CONV_W - 1):
                off = SUBLANES - (CONV_W - 1) + w
                out = out + cw_ref[w:w + 1, cols] * xp_ref[off:off + tm, cols]
            o_ref[:, cols] = _silu(out)


def _in_proj(x, w, conv_w, conv_b, *, seq, tm, tn):
    m, k = x.shape
    n = w.shape[1]
    n_gate_tiles = D_INNER // tn
    n_conv_tiles = CONV_DIM // tn
    conv_idx = lambda i, j: (0, jnp.maximum(j - n_gate_tiles, 0))
    return pl.pallas_call(
        functools.partial(_inproj_kernel, seq // tm, n_gate_tiles),
        grid=(m // tm, n // tn),
        in_specs=[pl.BlockSpec((tm, k), lambda i, j: (i, 0)),
                  pl.BlockSpec((k, tn), lambda i, j: (0, j)),
                  pl.BlockSpec((CONV_W, tn), conv_idx),
                  pl.BlockSpec((1, tn), conv_idx)],
        out_specs=pl.BlockSpec((tm, tn), lambda i, j: (i, j)),
        out_shape=jax.ShapeDtypeStruct((m, n), F32),
        scratch_shapes=[pltpu.VMEM((tm, k), BF16),
                        pltpu.VMEM((tm + SUBLANES, tn), F32),
                        pltpu.VMEM((n_conv_tiles, SUBLANES, tn), F32)],
        compiler_params=_params(("arbitrary", "arbitrary")),
        name="in_proj_conv",
    )(x, w, conv_w, conv_b)


def _qkv_kernel(x_ref, w_ref, cos_ref, sin_ref, o_ref):
    xb = x_ref[...].astype(BF16)
    cos = cos_ref[...]
    sin = sin_ref[...]
    for n in range(QKV_W // QKV_SUB):
        acc = jnp.dot(xb, w_ref[:, n * QKV_SUB:(n + 1) * QKV_SUB], preferred_element_type=F32)
        for c in range(QKV_SUB // LANES):
            blk = acc[:, c * LANES:(c + 1) * LANES]
            if n < ROPE_SUBTILES:
                blk = blk * cos + pltpu.roll(blk, LANES // 2, axis=1) * sin
            lo = n * QKV_SUB + c * LANES
            o_ref[:, lo:lo + LANES] = blk.astype(o_ref.dtype)


def _qkv_proj(x, w, cos, sin, *, seq, tm):
    m, k = x.shape
    tab = pl.BlockSpec((tm, LANES), lambda i: (i % (seq // tm), 0))
    return pl.pallas_call(
        _qkv_kernel,
        grid=(m // tm,),
        in_specs=[pl.BlockSpec((tm, k), lambda i: (i, 0)), _resident(w.shape), tab, tab],
        out_specs=pl.BlockSpec((tm, QKV_W), lambda i: (i, 0)),
        out_shape=jax.ShapeDtypeStruct((m, QKV_W), BF16),
        compiler_params=_params(("parallel",)),
        name="qkv_rope",
    )(x, w, cos, sin)


def _mm_ln_kernel(x_ref, w_ref, res_ref, g_ref, b_ref, o_ref, acc_ref):
    kk = pl.program_id(1)

    @pl.when(kk == 0)
    def _():
        acc_ref[...] = jnp.zeros_like(acc_ref)

    acc_ref[...] += jnp.dot(x_ref[...], w_ref[...], preferred_element_type=F32)

    @pl.when(kk == pl.num_programs(1) - 1)
    def _():
        v = DEEPNORM_ALPHA * res_ref[...] + acc_ref[...]
        o_ref[...] = _layernorm(v, g_ref[...], b_ref[...])


def _mm_res_ln(x, w, res, g, b, *, tm, tk):
    m, k = x.shape
    n = w.shape[1]
    return pl.pallas_call(
        _mm_ln_kernel,
        grid=(m // tm, k // tk),
        in_specs=[pl.BlockSpec((tm, tk), lambda i, kk: (i, kk)),
                  pl.BlockSpec((tk, n), lambda i, kk: (kk, 0)),
                  pl.BlockSpec((tm, n), lambda i, kk: (i, 0)),
                  pl.BlockSpec((1, n), lambda i, kk: (0, 0)),
                  pl.BlockSpec((1, n), lambda i, kk: (0, 0))],
        out_specs=pl.BlockSpec((tm, n), lambda i, kk: (i, 0)),
        out_shape=jax.ShapeDtypeStruct((m, n), F32),
        scratch_shapes=[pltpu.VMEM((tm, n), F32)],
        compiler_params=_params(("parallel", "arbitrary")),
        name="mm_res_ln",
    )(x, w, res, g, b)


def _mlp_kernel(x_ref, w1_ref, w2_ref, g_ref, b_ref, o_ref, xb_ref, acc_ref):
    f = pl.program_id(1)

    @pl.when(f == 0)
    def _():
        xb_ref[...] = x_ref[...].astype(BF16)
        acc_ref[...] = jnp.zeros_like(acc_ref)

    hid = jnp.dot(xb_ref[...], w1_ref[...], preferred_element_type=F32)
    hid = jnp.square(jnp.maximum(hid, 0.0)).astype(BF16)
    acc_ref[...] += jnp.dot(hid, w2_ref[...], preferred_element_type=F32)

    @pl.when(f == pl.num_programs(1) - 1)
    def _():
        v = DEEPNORM_ALPHA * x_ref[...] + acc_ref[...]
        o_ref[...] = _layernorm(v, g_ref[...], b_ref[...])


def _mlp_ln(x, w1, w2, g, b, *, tm, tf):
    m, d = x.shape
    dff = w1.shape[1]
    return pl.pallas_call(
        _mlp_kernel,
        grid=(m // tm, dff // tf),
        in_specs=[pl.BlockSpec((tm, d), lambda i, f: (i, 0)),
                  pl.BlockSpec((d, tf), lambda i, f: (0, f)),
                  pl.BlockSpec((tf, d), lambda i, f: (f, 0)),
                  pl.BlockSpec((1, d), lambda i, f: (0, 0)),
                  pl.BlockSpec((1, d), lambda i, f: (0, 0))],
        out_specs=pl.BlockSpec((tm, d), lambda i, f: (i, 0)),
        out_shape=jax.ShapeDtypeStruct((m, d), F32),
        scratch_shapes=[pltpu.VMEM((tm, d), BF16), pltpu.VMEM((tm, d), F32)],
        compiler_params=_params(("parallel", "arbitrary")),
        name="mlp_ln",
    )(x, w1, w2, g, b)


def _ssd_kernel(gz_ref, xs_ref, bm_ref, cm_ref, h_ref, wdt_ref,
                dtb_ref, alog_ref, dexp_ref, nw_ref, e3_ref, rall_ref,
                o_ref, state_ref, dtx_ref, dte_ref, ecs_ref):
    @pl.when(pl.program_id(1) == 0)
    def _():
        state_ref[...] = jnp.zeros_like(state_ref)

    pre = jnp.dot(h_ref[...].astype(BF16), wdt_ref[...], preferred_element_type=F32) + dtb_ref[...]
    dt = jnp.maximum(pre, 0.0) + jnp.log1p(jnp.exp(-jnp.abs(pre)))
    da = dt * (-jnp.exp(alog_ref[...]))
    row = lax.broadcasted_iota(jnp.int32, (CHUNK, CHUNK), 0)
    col = lax.broadcasted_iota(jnp.int32, (CHUNK, CHUNK), 1)
    causal = row >= col
    low_lanes = col < N_SSM_HEADS
    tril = jnp.where(causal, 1.0, 0.0).astype(BF16)
    cs = sum(jnp.dot(tril, p.astype(BF16), preferred_element_type=F32) for p in _split3(da))
    cs_t = cs.T
    to_end = jnp.exp(cs[CHUNK - 1:CHUNK, :] - cs)
    ecs = jnp.exp(cs)

    def pack(v):
        hi, mid, lo = _split3(v)
        a = jnp.where(low_lanes, hi, pltpu.roll(mid, N_SSM_HEADS, axis=1))
        b = jnp.where(low_lanes, lo, 0.0)
        return jnp.concatenate([a, b], axis=1).astype(BF16)

    def expand(v):
        return jnp.dot(pack(v), e3_ref[...], preferred_element_type=F32)

    dtx_ref[...] = expand(dt)
    dte_ref[...] = expand(dt * to_end)
    ecs_ref[...] = expand(ecs)
    cs_packed = pack(cs)

    for g in range(N_SSM_GROUPS):
        lo, hi = g * GROUP_W, (g + 1) * GROUP_W
        xs_g = xs_ref[:, lo:hi]
        b_g = bm_ref[:, g * D_STATE:(g + 1) * D_STATE]
        c_g = cm_ref[:, g * D_STATE:(g + 1) * D_STATE].astype(BF16)
        xdt_g = (xs_g * dtx_ref[:, lo:hi]).astype(BF16)
        xdte_g = (xs_g * dte_ref[:, lo:hi]).astype(BF16)
        cb = lax.dot_general(c_g, b_g.astype(BF16), (((1,), (1,)), ((), ())),
                             preferred_element_type=F32)
        rep = jnp.dot(cs_packed, rall_ref[g], preferred_element_type=F32)
        st = state_ref[g]
        y_g = jnp.dot(c_g, st.astype(BF16), preferred_element_type=F32) * ecs_ref[:, lo:hi]
        pieces = []
        for hh in range(HEADS_PER_GROUP):
            h = g * HEADS_PER_GROUP + hh
            seg = rep[:, hh * LANES:(hh + 1) * LANES] - cs_t[h:h + 1, :]
            decay = jnp.exp(jnp.where(causal, seg, -jnp.inf))
            mm = (cb * decay).astype(BF16)
            pieces.append(jnp.dot(mm, xdt_g[:, hh * SSM_HEADDIM:(hh + 1) * SSM_HEADDIM],
                                  preferred_element_type=F32))
        y_g = y_g + jnp.concatenate(pieces, axis=1)
        state_ref[g] = (st * ecs_ref[CHUNK - 1:CHUNK, lo:hi]
                        + jnp.dot(b_g.T.astype(BF16), xdte_g, preferred_element_type=F32))
        y_g = (y_g + dexp_ref[:, lo:hi] * xs_g) * gz_ref[:, lo:hi]
        ms = jnp.mean(y_g * y_g, axis=-1, keepdims=True)
        o_ref[:, lo:hi] = (y_g * lax.rsqrt(ms + RMS_EPS) * nw_ref[:, lo:hi]).astype(o_ref.dtype)


def _ssd(zx, h, wdt, dtb, alog, dexp, nw, e3, rall, *, bsz, seq):
    nc = seq // CHUNK
    rows = lambda b, c: b * nc + c

    def col(width, offset):
        return pl.BlockSpec((CHUNK, width), lambda b, c: (rows(b, c), offset // width))

    consts = (wdt, dtb, alog, dexp, nw, e3, rall)
    return pl.pallas_call(
        _ssd_kernel,
        grid=(bsz, nc),
        in_specs=[col(D_INNER, 0), col(D_INNER, D_INNER), col(GN, 2 * D_INNER),
                  col(GN, 2 * D_INNER + GN),
                  pl.BlockSpec((CHUNK, D_MODEL), lambda b, c: (rows(b, c), 0))]
                 + [_resident(a.shape) for a in consts],
        out_specs=pl.BlockSpec((CHUNK, D_INNER), lambda b, c: (rows(b, c), 0)),
        out_shape=jax.ShapeDtypeStruct((bsz * seq, D_INNER), BF16),
        scratch_shapes=[pltpu.VMEM((N_SSM_GROUPS, D_STATE, GROUP_W), F32),
                        pltpu.VMEM((CHUNK, D_INNER), F32),
                        pltpu.VMEM((CHUNK, D_INNER), F32),
                        pltpu.VMEM((CHUNK, D_INNER), F32)],
        compiler_params=_params(("parallel", "arbitrary")),
        name="ssd",
    )(zx, zx, zx, zx, h, *consts)


def _attn_kernel(sink_ref, q_ref, kp_ref, kc_ref, vp_ref, vc_ref, o_ref):
    blk = pl.program_id(1)
    rows = Q_PER_KV * WINDOW
    qi = lax.broadcasted_iota(jnp.int32, (rows, 2 * WINDOW), 0) & (WINDOW - 1)
    kc = lax.broadcasted_iota(jnp.int32, (rows, 2 * WINDOW), 1)
    dist = qi + WINDOW - kc
    valid = (dist >= 0) & (dist < WINDOW) & ((blk > 0) | (kc >= WINDOW))
    lane = lax.broadcasted_iota(jnp.int32, (WINDOW, LANES), 1)
    head_a = (lane & HALF) == 0
    mask_a = jnp.where(head_a, 1.0, 0.0).astype(BF16)
    mask_b = jnp.where(head_a, 0.0, 1.0).astype(BF16)
    left = lane < HEAD_DIM
    lane_kv = lax.broadcasted_iota(jnp.int32, (2 * WINDOW, LANES), 1)
    ones_l = jnp.where(lane_kv < HEAD_DIM, 1.0, 0.0).astype(BF16)
    ones_r = jnp.where(lane_kv < HEAD_DIM, 0.0, 1.0).astype(BF16)
    zeros_v = jnp.zeros((2 * WINDOW, HEAD_DIM), BF16)
    for g in range(N_KV_HEADS):
        kd = jnp.concatenate([kp_ref[:, g * LANES:(g + 1) * LANES],
                              kc_ref[:, g * LANES:(g + 1) * LANES]], axis=0)
        vb = jnp.concatenate([vp_ref[:, g * HEAD_DIM:(g + 1) * HEAD_DIM],
                              vc_ref[:, g * HEAD_DIM:(g + 1) * HEAD_DIM]], axis=0)
        v_a = jnp.concatenate([vb, zeros_v, ones_l], axis=1)
        v_b = jnp.concatenate([zeros_v, vb, ones_r], axis=1)
        qp = [q_ref[:, (2 * g + pr) * LANES:(2 * g + pr + 1) * LANES] for pr in range(2)]
        qst = jnp.concatenate([qp[0] * mask_a, qp[1] * mask_a, qp[0] * mask_b, qp[1] * mask_b],
                              axis=0)
        s = lax.dot_general(qst, kd, (((1,), (1,)), ((), ())),
                            preferred_element_type=F32) * ATTN_SCALE
        s = jnp.where(valid, s, -jnp.inf)
        sink = jnp.concatenate(
            [jnp.full((WINDOW, 1), sink_ref[g * Q_PER_KV + i], F32) for i in (0, 2, 1, 3)], axis=0)
        m = jnp.maximum(jnp.max(s, axis=-1, keepdims=True), sink)
        p = jnp.exp(s - m).astype(BF16)
        sink_term = jnp.exp(sink - m)
        out_a = jnp.dot(p[:2 * WINDOW], v_a, preferred_element_type=F32)
        out_b = jnp.dot(p[2 * WINDOW:], v_b, preferred_element_type=F32)
        for pr in range(2):
            r_a = slice(pr * WINDOW, (pr + 1) * WINDOW)
            r_b = slice((2 + pr) * WINDOW, (3 + pr) * WINDOW)
            num = out_a[r_a, :LANES] + out_b[r_a, :LANES]
            den = jnp.where(left, out_a[r_a, LANES:] + sink_term[r_a],
                            out_b[r_a, LANES:] + sink_term[r_b])
            o_ref[:, (2 * g + pr) * LANES:(2 * g + pr + 1) * LANES] = (num / den).astype(o_ref.dtype)


def _attention(qkv, sinks, *, bsz, seq):
    nb = seq // WINDOW
    rows = lambda b, i: b * nb + i
    prev = lambda b, i: b * nb + jnp.maximum(i - 1, 0)
    k_idx = D_MODEL // K_DUP_W
    v_idx = (D_MODEL + K_DUP_W) // KV_WIDTH
    return pl.pallas_call(
        _attn_kernel,
        grid=(bsz, nb),
        in_specs=[pl.BlockSpec(memory_space=pltpu.SMEM),
                  pl.BlockSpec((WINDOW, D_MODEL), lambda b, i: (rows(b, i), 0)),
                  pl.BlockSpec((WINDOW, K_DUP_W), lambda b, i: (prev(b, i), k_idx)),
                  pl.BlockSpec((WINDOW, K_DUP_W), lambda b, i: (rows(b, i), k_idx)),
                  pl.BlockSpec((WINDOW, KV_WIDTH), lambda b, i: (prev(b, i), v_idx)),
                  pl.BlockSpec((WINDOW, KV_WIDTH), lambda b, i: (rows(b, i), v_idx))],
        out_specs=pl.BlockSpec((WINDOW, D_MODEL), lambda b, i: (rows(b, i), 0)),
        out_shape=jax.ShapeDtypeStruct((bsz * seq, D_MODEL), BF16),
        compiler_params=_params(("parallel", "arbitrary")),
        name="swa_attn",
    )(sinks, qkv, qkv, qkv, qkv, qkv)


def _rope_tables(seq):
    inv_freq = ROPE_THETA ** (-jnp.arange(0, HEAD_DIM, 2, dtype=F32) / HEAD_DIM)
    ang = jnp.arange(seq, dtype=F32)[:, None] * inv_freq[None, :]
    cos, sin = jnp.cos(ang), jnp.sin(ang)
    cos_t = jnp.tile(cos, (1, LANES // HALF))
    sin_t = jnp.concatenate([-sin, -sin, sin, sin], axis=1)
    return cos_t, sin_t


def _qk_column_perms():
    c = np.arange(LANES)
    half, d = c // HEAD_DIM, c % HALF
    head_in_pair = (c // HALF) % 2
    q_cols = np.concatenate([(2 * p + head_in_pair) * HEAD_DIM + half * HALF + d
                             for p in range(N_Q_HEADS // 2)])
    k_cols = np.concatenate([g * HEAD_DIM + half * HALF + d for g in range(N_KV_HEADS)])
    return q_cols, k_cols


def _expand_tables():
    part_head = np.arange(2 * LANES) % N_SSM_HEADS
    used = np.arange(2 * LANES) < 3 * N_SSM_HEADS
    chan_head = np.arange(D_INNER) // SSM_HEADDIM
    e3 = (part_head[:, None] == chan_head[None, :]) & used[:, None]
    rep_head = np.arange(HEADS_PER_GROUP * LANES) // LANES
    rall = np.stack([(part_head[:, None] == g * HEADS_PER_GROUP + rep_head[None, :]) & used[:, None]
                     for g in range(N_SSM_GROUPS)])
    return jnp.asarray(e3, BF16), jnp.asarray(rall, BF16)


def _pad_lanes(v):
    return jnp.concatenate([v, jnp.zeros((LANES - v.shape[0],), F32)])[None, :]


def kernel(x, a_w_in, a_conv_w, a_conv_b, a_dt_bias, a_A_log, a_D, a_norm_w, a_w_out, kv_w, b_w_q, b_sinks, b_w_o, mlp_w1, mlp_w2, ln_g, ln_b):
    bsz, seq, d = x.shape
    t = bsz * seq
    h = x.reshape(t, d)
    cos_t, sin_t = _rope_tables(seq)
    e3, rall = _expand_tables()
    q_cols, k_cols = _qk_column_perms()
    kv_cols = None
    for layer in range(DEPTH):
        g0, b0 = ln_g[layer, 0][None, :], ln_b[layer, 0][None, :]
        g1, b1 = ln_g[layer, 1][None, :], ln_b[layer, 1][None, :]
        if layer < N_A_LAYERS:
            w_in = a_w_in[layer]
            zx = _in_proj(h, w_in[:, :ZX_W].astype(BF16), a_conv_w[layer], a_conv_b[layer][None, :],
                          seq=seq, tm=512, tn=1024)
            w_dt = jnp.concatenate([w_in[:, ZX_W:].astype(BF16),
                                    jnp.zeros((d, LANES - N_SSM_HEADS), BF16)], axis=1)
            y = _ssd(zx, h, w_dt, _pad_lanes(a_dt_bias[layer]), _pad_lanes(a_A_log[layer]),
                     jnp.repeat(a_D[layer], SSM_HEADDIM)[None, :], a_norm_w[layer][None, :],
                     e3, rall, bsz=bsz, seq=seq)
            h = _mm_res_ln(y, a_w_out[layer].astype(BF16), h, g0, b0, tm=512, tk=512)
        else:
            j = layer - N_A_LAYERS
            w_q = b_w_q[j][:, q_cols].astype(BF16)
            if layer == N_A_LAYERS:
                w_kv = jnp.concatenate([kv_w[:, :KV_WIDTH][:, k_cols], kv_w[:, KV_WIDTH:]],
                                       axis=1).astype(BF16)
                qkv = _qkv_proj(h, jnp.concatenate([w_q, w_kv], axis=1), cos_t, sin_t, seq=seq, tm=512)
                kv_cols = qkv[:, D_MODEL:]
            else:
                q_only = _qkv_proj(h, jnp.concatenate([w_q, jnp.zeros_like(w_kv)], axis=1),
                                   cos_t, sin_t, seq=seq, tm=512)
                qkv = jnp.concatenate([q_only[:, :D_MODEL], kv_cols], axis=1)
            o = _attention(qkv, b_sinks[j], bsz=bsz, seq=seq)
            h = _mm_res_ln(o, b_w_o[j].astype(BF16), h, g0, b0, tm=512, tk=512)
        h = _mlp_ln(h, mlp_w1[layer].astype(BF16), mlp_w2[layer].astype(BF16), g1, b1,
                    tm=512, tf=512)
    return h.reshape(bsz, seq, d)
```
